```python
import math
import jax, jax.numpy as jnp
from jax import lax
import numpy as np

D_MODEL = 2048
BATCH = 2
SEQ = 4096
DEPTH = 2

HEAD_DIM = 128
NORM_EPS = 1e-6
ROPE_THETA = 500000.0
ROPE_FRACTION = 4
Q_BLOCK = 128
A_HEADS = 6
A_QK = 64
A_V = 2 * A_QK
A_WIDTH = A_HEADS * A_V
B_HEADS = 5
B_WIDTH = B_HEADS * HEAD_DIM
MOBA_BLOCK = 256
MOBA_TOPK = 3
MOBA_Q_BLOCK = 64
C_HEADS = 5
C_WIDTH = C_HEADS * HEAD_DIM
N_BRANCH = 3
SPLIT_SIZES = (A_HEADS * 2 * A_QK, A_HEADS * 2 * A_QK, A_WIDTH,
               B_WIDTH, B_WIDTH, B_WIDTH,
               C_WIDTH, C_WIDTH, C_WIDTH, C_HEADS,
               N_BRANCH * D_MODEL)
IN_COLS = 3 * A_WIDTH + 3 * B_WIDTH + 3 * C_WIDTH + C_HEADS + N_BRANCH * D_MODEL
PEER_HEADS = 8
PEER_KEYS = 128
PEER_EXPERTS = PEER_KEYS * PEER_KEYS
PEER_QDIM = 256
PEER_TOPK = 16
PEER_TOKEN_BLOCK = 128

kernel_name = "hybrid_gated_diff_moba_fox_peer"


def rmsnorm(x, g):
    xf = x.astype(jnp.float32)
    y = xf * lax.rsqrt(jnp.mean(xf * xf, axis=-1, keepdims=True) + NORM_EPS)
    return y.astype(x.dtype) * g


def rope_partial(x, pos):
    dh = x.shape[-1]
    rd = dh // ROPE_FRACTION
    half = rd // 2
    inv = ROPE_THETA ** (-(jnp.arange(half, dtype=jnp.float32) * 2.0) / rd)
    ang = pos.astype(jnp.float32)[:, None] * inv[None, :]
    cos = jnp.cos(ang).astype(x.dtype)
    sin = jnp.sin(ang).astype(x.dtype)
    x1 = x[..., :half]
    x2 = x[..., half:rd]
    return jnp.concatenate([x1 * cos - x2 * sin, x2 * cos + x1 * sin, x[..., rd:]], axis=-1)


def _blocks_to_seq(o):
    nc, b, h, qb, d = o.shape
    return jnp.moveaxis(o, 0, 2).reshape(b, h, nc * qb, d)


def diff_attention(q, k, v, lam, out_g, lam_init):
    seq = q.shape[3]
    scale = A_QK ** -0.5
    kpos = jnp.arange(seq)

    def block(c):
        start = c * Q_BLOCK
        qc = lax.dynamic_slice_in_dim(q, start, Q_BLOCK, axis=3)
        logits = jnp.einsum('bhcqd,bhcsd->bhcqs', qc, k).astype(jnp.float32) * scale
        qpos = start + jnp.arange(Q_BLOCK)
        logits = jnp.where(kpos[None, :] <= qpos[:, None], logits, -jnp.inf)
        p = jax.nn.softmax(logits, axis=-1)
        w = p[:, :, 0] - lam * p[:, :, 1]
        return jnp.einsum('bhqs,bhsd->bhqd', w.astype(v.dtype), v)

    o = _blocks_to_seq(lax.map(block, jnp.arange(seq // Q_BLOCK)))
    return rmsnorm(o, out_g) * (1.0 - lam_init)


def moba_attention(q, k, v):
    bsz, nh, seq, dh = q.shape
    scale = dh ** -0.5
    pad = (-seq) % MOBA_BLOCK
    k_p = jnp.pad(k, ((0, 0), (0, 0), (0, pad), (0, 0)))
    v_p = jnp.pad(v, ((0, 0), (0, 0), (0, pad), (0, 0)))
    nb = (seq + pad) // MOBA_BLOCK
    k_blk = k_p.reshape(bsz, nh, nb, MOBA_BLOCK, dh)
    v_blk = v_p.reshape(bsz, nh, nb, MOBA_BLOCK, dh)
    k_mean = jnp.mean(k_blk.astype(jnp.float32), axis=3).astype(k.dtype)
    topk = min(MOBA_TOPK, nb)
    blocks = jnp.arange(nb)
    b_idx = jnp.arange(bsz)[:, None, None, None]
    h_idx = jnp.arange(nh)[None, :, None, None]

    def block(c):
        start = c * MOBA_Q_BLOCK
        own = start // MOBA_BLOCK
        qc = lax.dynamic_slice_in_dim(q, start, MOBA_Q_BLOCK, axis=2)
        qpos = start + jnp.arange(MOBA_Q_BLOCK)
        gate = jnp.einsum('bhqd,bhnd->bhqn', qc, k_mean).astype(jnp.float32)
        gate = jnp.where(blocks < own, gate, -jnp.inf)
        _, sel = lax.top_k(gate, topk)
        valid = sel < own
        k_sel = k_blk[b_idx, h_idx, sel]
        v_sel = v_blk[b_idx, h_idx, sel]
        l_sel = jnp.einsum('bhqd,bhqnsd->bhqns', qc, k_sel).astype(jnp.float32) * scale
        l_sel = jnp.where(valid[..., None], l_sel, -jnp.inf)
        l_sel = l_sel.reshape(bsz, nh, MOBA_Q_BLOCK, topk * MOBA_BLOCK)
        k_own = lax.dynamic_slice_in_dim(k_p, own * MOBA_BLOCK, MOBA_BLOCK, axis=2)
        v_own = lax.dynamic_slice_in_dim(v_p, own * MOBA_BLOCK, MOBA_BLOCK, axis=2)
        l_own = jnp.einsum('bhqd,bhsd->bhqs', qc, k_own).astype(jnp.float32) * scale
        opos = own * MOBA_BLOCK + jnp.arange(MOBA_BLOCK)
        l_own = jnp.where(opos[None, :] <= qpos[:, None], l_own, -jnp.inf)
        p = jax.nn.softmax(jnp.concatenate([l_sel, l_own], axis=-1), axis=-1).astype(v.dtype)
        p_sel = p[..., :topk * MOBA_BLOCK].reshape(bsz, nh, MOBA_Q_BLOCK, topk, MOBA_BLOCK)
        p_own = p[..., topk * MOBA_BLOCK:]
        return (jnp.einsum('bhqns,bhqnsd->bhqd', p_sel, v_sel)
                + jnp.einsum('bhqs,bhsd->bhqd', p_own, v_own))

    return _blocks_to_seq(lax.map(block, jnp.arange(seq // MOBA_Q_BLOCK)))


def forgetting_attention(q, k, v, log_f):
    seq = q.shape[2]
    scale = q.shape[-1] ** -0.5
    cum = jnp.cumsum(log_f, axis=-1)
    kpos = jnp.arange(seq)

    def block(c):
        start = c * Q_BLOCK
        qc = lax.dynamic_slice_in_dim(q, start, Q_BLOCK, axis=2)
        cq = lax.dynamic_slice_in_dim(cum, start, Q_BLOCK, axis=2)
        logits = jnp.einsum('bhqd,bhsd->bhqs', qc, k).astype(jnp.float32) * scale
        logits = logits + cq[..., :, None] - cum[..., None, :]
        qpos = start + jnp.arange(Q_BLOCK)
        logits = jnp.where(kpos[None, :] <= qpos[:, None], logits, -jnp.inf)
        p = jax.nn.softmax(logits, axis=-1).astype(v.dtype)
        return jnp.einsum('bhqs,bhsd->bhqd', p, v)

    return _blocks_to_seq(lax.map(block, jnp.arange(seq // Q_BLOCK)))


def hybrid_mixer(x, layer_idx, norm_g, w_in, b_gate, b_forget, a_q_norm, a_k_norm,
                 lam_q1, lam_k1, lam_q2, lam_k2, a_out_norm, b_q_norm, b_k_norm,
                 c_q_norm, c_k_norm, w_br_a, w_br_b, w_br_c, w_out):
    bsz, seq, _ = x.shape
    pos = jnp.arange(seq)
    h = rmsnorm(x, norm_g)
    proj = h @ w_in
    offs = np.cumsum(SPLIT_SIZES)[:-1].tolist()
    qa, ka, va, qb, kb, vb, qc, kc, vc, f_logit, g_logit = jnp.split(proj, offs, axis=-1)

    def to_heads(t, n):
        return t.reshape(bsz, seq, n, -1).transpose(0, 2, 1, 3)

    def from_heads(t):
        return t.transpose(0, 2, 1, 3).reshape(bsz, seq, -1)

    def split_a(t):
        return t.reshape(bsz, seq, A_HEADS, 2, A_QK).transpose(0, 2, 3, 1, 4)
    qa = rope_partial(rmsnorm(split_a(qa), a_q_norm), pos)
    ka = rope_partial(rmsnorm(split_a(ka), a_k_norm), pos)
    lam_init = 0.8 - 0.6 * math.exp(-0.3 * layer_idx)
    f32 = jnp.float32
    lam = (jnp.exp(jnp.sum(lam_q1.astype(f32) * lam_k1.astype(f32)))
           - jnp.exp(jnp.sum(lam_q2.astype(f32) * lam_k2.astype(f32))) + lam_init)
    ya = from_heads(diff_attention(qa, ka, to_heads(va, A_HEADS), lam, a_out_norm, lam_init))

    qb = rope_partial(rmsnorm(to_heads(qb, B_HEADS), b_q_norm), pos)
    kb = rope_partial(rmsnorm(to_heads(kb, B_HEADS), b_k_norm), pos)
    yb = from_heads(moba_attention(qb, kb, to_heads(vb, B_HEADS)))

    qc = rmsnorm(to_heads(qc, C_HEADS), c_q_norm)
    kc = rmsnorm(to_heads(kc, C_HEADS), c_k_norm)
    log_f = jax.nn.log_sigmoid((f_logit + b_forget).astype(f32)).transpose(0, 2, 1)
    yc = from_heads(forgetting_attention(qc, kc, to_heads(vc, C_HEADS), log_f))

    gates = jax.nn.sigmoid(g_logit.reshape(bsz, seq, N_BRANCH, D_MODEL) + b_gate)
    merged = (gates[:, :, 0] * (ya @ w_br_a)
              + gates[:, :, 1] * (yb @ w_br_b)
              + gates[:, :, 2] * (yc @ w_br_c))
    return merged @ w_out


def peer_ffn(h, w_q, sub_keys, u_tab, v_tab):
    bsz, seq, dm = h.shape
    tokens = bsz * seq
    hf = h.reshape(tokens, dm)
    q = (hf @ w_q).reshape(tokens, PEER_HEADS, 2, PEER_QDIM // 2)
    s = jnp.einsum('thpd,hpnd->thpn', q, sub_keys).astype(jnp.float32)
    sv, si = lax.top_k(s, PEER_TOPK)
    cand = sv[:, :, 0, :, None] + sv[:, :, 1, None, :]
    cand_idx = si[:, :, 0, :, None] * PEER_KEYS + si[:, :, 1, None, :]
    top_s, top_pos = lax.top_k(cand.reshape(tokens, PEER_HEADS, PEER_TOPK * PEER_TOPK), PEER_TOPK)
    expert = jnp.take_along_axis(cand_idx.reshape(tokens, PEER_HEADS, PEER_TOPK * PEER_TOPK),
                                 top_pos, axis=-1)
    gate = jax.nn.softmax(top_s, axis=-1).astype(h.dtype)
    nch = tokens // PEER_TOKEN_BLOCK

    def block(args):
        xb, eb, gb = args
        u = u_tab[eb]
        act = jax.nn.gelu(jnp.einsum('td,thkd->thk', xb, u), approximate=False)
        return jnp.einsum('thk,thkd->td', gb * act, v_tab[eb])

    y = lax.map(block, (hf.reshape(nch, PEER_TOKEN_BLOCK, dm),
                        expert.reshape(nch, PEER_TOKEN_BLOCK, PEER_HEADS, PEER_TOPK),
                        gate.reshape(nch, PEER_TOKEN_BLOCK, PEER_HEADS, PEER_TOPK)))
    return y.reshape(bsz, seq, dm)


def setup_inputs(seed: int = 0) -> dict:
    key = jax.random.key(seed)
    ks = jax.random.split(key, 25)
    f32 = jnp.float32
    L = DEPTH
    D = D_MODEL

    def nrm(k, shape, scale):
        return jax.random.normal(k, shape, f32) * scale

    def gain(k, shape):
        return 1.0 + 0.1 * jax.random.normal(k, shape, f32)

    return {
        "x": nrm(ks[0], (BATCH, SEQ, D), 1.0),
        "norm1_g": gain(ks[1], (L, D)),
        "w_in": nrm(ks[2], (L, D, IN_COLS), D ** -0.5),
        "b_gate": nrm(ks[3], (L, N_BRANCH, D), 0.1),
        "b_forget": nrm(ks[4], (L, C_HEADS), 0.1),
        "a_q_norm": gain(ks[5], (L, A_QK)),
        "a_k_norm": gain(ks[6], (L, A_QK)),
        "lambda_q1": nrm(ks[7], (L, A_QK), 0.1),
        "lambda_k1": nrm(ks[8], (L, A_QK), 0.1),
        "lambda_q2": nrm(ks[9], (L, A_QK), 0.1),
        "lambda_k2": nrm(ks[10], (L, A_QK), 0.1),
        "a_out_norm": gain(ks[11], (L, A_V)),
        "b_q_norm": gain(ks[12], (L, HEAD_DIM)),
        "b_k_norm": gain(ks[13], (L, HEAD_DIM)),
        "c_q_norm": gain(ks[14], (L, HEAD_DIM)),
        "c_k_norm": gain(ks[15], (L, HEAD_DIM)),
        "w_branch_a": nrm(ks[16], (L, A_WIDTH, D), A_WIDTH ** -0.5),
        "w_branch_b": nrm(ks[17], (L, B_WIDTH, D), B_WIDTH ** -0.5),
        "w_branch_c": nrm(ks[18], (L, C_WIDTH, D), C_WIDTH ** -0.5),
        "w_out": nrm(ks[19], (L, D, D), D ** -0.5),
        "norm2_g": gain(ks[20], (L, D)),
        "w_peer_q": nrm(ks[21], (L, D, PEER_HEADS * PEER_QDIM), D ** -0.5),
        "peer_sub_keys": nrm(ks[22], (L, PEER_HEADS, 2, PEER_KEYS, PEER_QDIM // 2), (PEER_QDIM // 2) ** -0.5),
        "peer_u": nrm(ks[23], (L, PEER_EXPERTS, D), D ** -0.5),
        "peer_v": nrm(ks[24], (L, PEER_EXPERTS, D), PEER_HEADS ** -0.5),
    }


def reference(x, norm1_g, w_in, b_gate, b_forget, a_q_norm, a_k_norm, lambda_q1, lambda_k1,
              lambda_q2, lambda_k2, a_out_norm, b_q_norm, b_k_norm, c_q_norm, c_k_norm,
              w_branch_a, w_branch_b, w_branch_c, w_out, norm2_g, w_peer_q, peer_sub_keys,
              peer_u, peer_v):
    for l in range(DEPTH):
        x = x + hybrid_mixer(x, l, norm1_g[l], w_in[l], b_gate[l], b_forget[l],
                             a_q_norm[l], a_k_norm[l], lambda_q1[l], lambda_k1[l],
                             lambda_q2[l], lambda_k2[l], a_out_norm[l], b_q_norm[l], b_k_norm[l],
                             c_q_norm[l], c_k_norm[l], w_branch_a[l], w_branch_b[l],
                             w_branch_c[l], w_out[l])
        x = x + peer_ffn(rmsnorm(x, norm2_g[l]), w_peer_q[l], peer_sub_keys[l],
                         peer_u[l], peer_v[l])
    return x
```

```python
import functools
import math

import jax
import jax.numpy as jnp
from jax import lax
from jax.experimental import pallas as pl
from jax.experimental.pallas import tpu as pltpu

D_MODEL = 2048
HEAD_DIM = 128
NORM_EPS = 1e-6
ROPE_THETA = 500000.0
ROPE_FRACTION = 4
A_HEADS = 6
A_QK = 64
B_HEADS = 5
C_HEADS = 5
MOBA_BLOCK = 256
MOBA_TOPK = 3
N_BRANCH = 3
PEER_HEADS = 8
PEER_KEYS = 128
PEER_TOPK = 16

LANES = 128
VMEM_LIMIT = 56 * 1024 * 1024
ATT_TILE = 256
NEG_INF = float("-inf")

_NT = (((1,), (1,)), ((), ()))
_TN = (((0,), (0,)), ((), ()))


def _params(*sem):
    return pltpu.CompilerParams(dimension_semantics=sem, vmem_limit_bytes=VMEM_LIMIT)


def _rmsnorm_kernel(x_ref, g_ref, o_ref):
    x = x_ref[...]
    ms = jnp.mean(x * x, axis=-1, keepdims=True)
    o_ref[...] = (x * lax.rsqrt(ms + NORM_EPS) * g_ref[...]).astype(o_ref.dtype)


def rmsnorm_bf16(x2d, g):
    t, d = x2d.shape
    tm = min(512, t)
    return pl.pallas_call(
        _rmsnorm_kernel,
        grid=(t // tm,),
        in_specs=[pl.BlockSpec((tm, d), lambda i: (i, 0)),
                  pl.BlockSpec((1, d), lambda i: (0, 0))],
        out_specs=pl.BlockSpec((tm, d), lambda i: (i, 0)),
        out_shape=jax.ShapeDtypeStruct((t, d), jnp.bfloat16),
        compiler_params=_params("parallel"),
        name="rmsnorm",
    )(x2d, g.reshape(1, d))


def _mm_tiles(t, n, tn):
    tm = min(1024, t)
    assert t % tm == 0 and n % tn == 0
    return tm, tn


def _mm_plain_kernel(h_ref, w_ref, o_ref):
    acc = jnp.dot(h_ref[...], w_ref[...], preferred_element_type=jnp.float32)
    o_ref[...] = acc.astype(o_ref.dtype)


def mm_plain(h, w, tn=512):
    t, k = h.shape
    n = w.shape[1]
    tm, tn = _mm_tiles(t, n, tn)
    return pl.pallas_call(
        _mm_plain_kernel,
        grid=(t // tm, n // tn),
        in_specs=[pl.BlockSpec((tm, k), lambda i, j: (i, 0)),
                  pl.BlockSpec((k, tn), lambda i, j: (0, j))],
        out_specs=pl.BlockSpec((tm, tn), lambda i, j: (i, j)),
        out_shape=jax.ShapeDtypeStruct((t, n), jnp.bfloat16),
        compiler_params=_params("parallel", "parallel"),
        name="mm_plain",
    )(h, w)


def _mm_gate_kernel(h_ref, w_ref, b_ref, o_ref):
    acc = jnp.dot(h_ref[...], w_ref[...], preferred_element_type=jnp.float32)
    o_ref[...] = jax.nn.sigmoid(acc + b_ref[...]).astype(o_ref.dtype)


def mm_gate(h, w, b, tn=512):
    t, k = h.shape
    n = w.shape[1]
    tm, tn = _mm_tiles(t, n, tn)
    return pl.pallas_call(
        _mm_gate_kernel,
        grid=(t // tm, n // tn),
        in_specs=[pl.BlockSpec((tm, k), lambda i, j: (i, 0)),
                  pl.BlockSpec((k, tn), lambda i, j: (0, j)),
                  pl.BlockSpec((1, tn), lambda i, j: (0, j))],
        out_specs=pl.BlockSpec((tm, tn), lambda i, j: (i, j)),
        out_shape=jax.ShapeDtypeStruct((t, n), jnp.bfloat16),
        compiler_params=_params("parallel", "parallel"),
        name="mm_gate",
    )(h, w, b.reshape(1, n))


def _mm_resid_kernel(h_ref, w_ref, r_ref, o_ref):
    acc = jnp.dot(h_ref[...], w_ref[...], preferred_element_type=jnp.float32)
    o_ref[...] = r_ref[...] + acc


def mm_resid(h, w, resid, tn=512):
    t, k = h.shape
    n = w.shape[1]
    tm, tn = _mm_tiles(t, n, tn)
    return pl.pallas_call(
        _mm_resid_kernel,
        grid=(t // tm, n // tn),
        in_specs=[pl.BlockSpec((tm, k), lambda i, j: (i, 0)),
                  pl.BlockSpec((k, tn), lambda i, j: (0, j)),
                  pl.BlockSpec((tm, tn), lambda i, j: (i, j))],
        out_specs=pl.BlockSpec((tm, tn), lambda i, j: (i, j)),
        out_shape=jax.ShapeDtypeStruct((t, n), jnp.float32),
        compiler_params=_params("parallel", "parallel"),
        name="mm_resid",
    )(h, w, resid)


def _qk_epilogue(y, gain, group, rope, cos, sp, sm):
    ss = y * y
    if group == LANES:
        ms = jnp.mean(ss, axis=-1, keepdims=True)
    else:
        lane = lax.broadcasted_iota(jnp.int32, y.shape, 1)
        lo = lane < group
        s_lo = jnp.sum(jnp.where(lo, ss, 0.0), axis=-1, keepdims=True)
        s_hi = jnp.sum(jnp.where(lo, 0.0, ss), axis=-1, keepdims=True)
        ms = jnp.where(lo, s_lo, s_hi) * (1.0 / group)
    yn = y * lax.rsqrt(ms + NORM_EPS) * gain
    if rope:
        half = group // ROPE_FRACTION // 2
        yn = (yn * cos + pltpu.roll(yn, LANES - half, 1) * sp
              + pltpu.roll(yn, half, 1) * sm)
    return yn


def _mm_qk_kernel(h_ref, w_ref, g_ref, cos_ref, sp_ref, sm_ref, o_ref, *maybe_kmean,
                  group, rope):
    acc = jnp.dot(h_ref[...], w_ref[...], preferred_element_type=jnp.float32)
    tm, tn = acc.shape
    cos, sp, sm = cos_ref[...], sp_ref[...], sm_ref[...]
    for b in range(tn // LANES):
        cols = slice(b * LANES, (b + 1) * LANES)
        yn = _qk_epilogue(acc[:, cols], g_ref[:, cols], group, rope, cos, sp, sm)
        o_ref[:, cols] = yn.astype(o_ref.dtype)
        if maybe_kmean:
            kmean_ref, = maybe_kmean
            for r in range(tm // MOBA_BLOCK):
                blk = yn[r * MOBA_BLOCK:(r + 1) * MOBA_BLOCK]
                kmean_ref[0, r:r + 1, cols] = jnp.mean(blk, axis=0, keepdims=True)


def mm_qk(h, w, gain_row, tables, seq, group, rope, with_kmean, tn):
    t, k = h.shape
    n = w.shape[1]
    tm = min(1024, seq)
    assert seq % tm == 0 and n % tn == 0 and tm % MOBA_BLOCK == 0
    spt = seq // tm
    cos, sp, sm = tables
    tab_spec = pl.BlockSpec((tm, LANES), lambda i, j: (i % spt, 0))
    out_shape = [jax.ShapeDtypeStruct((t, n), jnp.bfloat16)]
    out_specs = [pl.BlockSpec((tm, tn), lambda i, j: (i, j))]
    if with_kmean:
        nb = tm // MOBA_BLOCK
        out_shape.append(jax.ShapeDtypeStruct((t // tm, nb, n), jnp.float32))
        out_specs.append(pl.BlockSpec((1, nb, tn), lambda i, j: (i, 0, j)))
    outs = pl.pallas_call(
        functools.partial(_mm_qk_kernel, group=group, rope=rope),
        grid=(t // tm, n // tn),
        in_specs=[pl.BlockSpec((tm, k), lambda i, j: (i, 0)),
                  pl.BlockSpec((k, tn), lambda i, j: (0, j)),
                  pl.BlockSpec((1, tn), lambda i, j: (0, j)),
                  tab_spec, tab_spec, tab_spec],
        out_specs=out_specs,
        out_shape=out_shape,
        compiler_params=_params("parallel", "parallel"),
        name="mm_qk",
    )(h, w, gain_row, cos, sp, sm)
    if with_kmean:
        return outs[0], outs[1].reshape(t // MOBA_BLOCK, n)
    return outs[0]


def _mm_forget_kernel(h_ref, w_ref, b_ref, cum_ref, cumt_ref, carry_ref, *, spt):
    i = pl.program_id(0)

    @pl.when(i % spt == 0)
    def _():
        carry_ref[...] = jnp.zeros_like(carry_ref)

    acc = jnp.dot(h_ref[...], w_ref[...], preferred_element_type=jnp.float32)
    z = acc + b_ref[...]
    logf = jnp.minimum(z, 0.0) - jnp.log1p(jnp.exp(-jnp.abs(z)))
    tm = logf.shape[0]
    blk = MOBA_BLOCK
    row = lax.broadcasted_iota(jnp.int32, (blk, blk), 0)
    col = lax.broadcasted_iota(jnp.int32, (blk, blk), 1)
    tri = (col <= row).astype(jnp.float32)
    carry = carry_ref[...]
    for r in range(tm // blk):
        c = jnp.dot(tri, logf[r * blk:(r + 1) * blk], preferred_element_type=jnp.float32,
                    precision=lax.Precision.HIGHEST) + carry
        cum_ref[r * blk:(r + 1) * blk, :] = c
        cumt_ref[:, r * blk:(r + 1) * blk] = c.T[:8]
        carry = c[blk - 1:blk]
    carry_ref[...] = carry


def mm_forget(h, w_pad, b_pad, seq):
    t, k = h.shape
    tm = min(1024, seq)
    spt = seq // tm
    return pl.pallas_call(
        functools.partial(_mm_forget_kernel, spt=spt),
        grid=(t // tm,),
        in_specs=[pl.BlockSpec((tm, k), lambda i: (i, 0)),
                  pl.BlockSpec((k, LANES), lambda i: (0, 0)),
                  pl.BlockSpec((1, LANES), lambda i: (0, 0))],
        out_specs=[pl.BlockSpec((tm, LANES), lambda i: (i, 0)),
                   pl.BlockSpec((8, tm), lambda i: (0, i))],
        out_shape=[jax.ShapeDtypeStruct((t, LANES), jnp.float32),
                   jax.ShapeDtypeStruct((8, t), jnp.float32)],
        scratch_shapes=[pltpu.VMEM((1, LANES), jnp.float32)],
        compiler_params=_params("arbitrary"),
        name="mm_forget",
    )(h, w_pad, b_pad)


def _softmax_step(s, v, m_ref, l_ref, acc_ref):
    m_old = m_ref[...]
    m_new = jnp.maximum(m_old, jnp.max(s, axis=-1, keepdims=True))
    alpha = jnp.exp(m_old - m_new)
    p = jnp.exp(s - m_new)
    l_ref[...] = alpha * l_ref[...] + jnp.sum(p, axis=-1, keepdims=True)
    acc_ref[...] = alpha * acc_ref[...] + jnp.dot(
        p.astype(v.dtype), v, preferred_element_type=jnp.float32)
    m_ref[...] = m_new


def _init_softmax(m_ref, l_ref, acc_ref):
    m_ref[...] = jnp.full(m_ref.shape, NEG_INF, jnp.float32)
    l_ref[...] = jnp.zeros(l_ref.shape, jnp.float32)
    acc_ref[...] = jnp.zeros(acc_ref.shape, jnp.float32)


def _causal_mask(s):
    row = lax.broadcasted_iota(jnp.int32, s.shape, 0)
    col = lax.broadcasted_iota(jnp.int32, s.shape, 1)
    return jnp.where(col <= row, s, NEG_INF)


def _kv_block(ref, j):
    return ref[pl.ds(pl.multiple_of(j * ATT_TILE, ATT_TILE), ATT_TILE), :]


def _diff_attn_kernel(q_ref, k_ref, v_ref, lq1_ref, lk1_ref, lq2_ref, lk2_ref, og_ref, o_ref,
                      m1_ref, l1_ref, a1_ref, m2_ref, l2_ref, a2_ref, *, lam_init):
    qi = pl.program_id(2)
    scale = A_QK ** -0.5
    q = q_ref[...]
    lane = lax.broadcasted_iota(jnp.int32, q.shape, 1)
    zero = jnp.zeros_like(q)
    q1 = jnp.where(lane < A_QK, q, zero)
    q2 = jnp.where(lane < A_QK, zero, q)
    _init_softmax(m1_ref, l1_ref, a1_ref)
    _init_softmax(m2_ref, l2_ref, a2_ref)

    def step(j, masked):
        kb = _kv_block(k_ref, j)
        vb = _kv_block(v_ref, j)
        s1 = lax.dot_general(q1, kb, _NT, preferred_element_type=jnp.float32) * scale
        s2 = lax.dot_general(q2, kb, _NT, preferred_element_type=jnp.float32) * scale
        if masked:
            s1, s2 = _causal_mask(s1), _causal_mask(s2)
        _softmax_step(s1, vb, m1_ref, l1_ref, a1_ref)
        _softmax_step(s2, vb, m2_ref, l2_ref, a2_ref)

    def body(j, c):
        step(j, False)
        return c

    lax.fori_loop(0, qi, body, 0)
    step(qi, True)

    lam = (jnp.exp(jnp.sum(lq1_ref[...] * lk1_ref[...], axis=-1, keepdims=True))
           - jnp.exp(jnp.sum(lq2_ref[...] * lk2_ref[...], axis=-1, keepdims=True)) + lam_init)
    o = a1_ref[...] / l1_ref[...] - lam * (a2_ref[...] / l2_ref[...])
    ms = jnp.mean(o * o, axis=-1, keepdims=True)
    o = o * lax.rsqrt(ms + NORM_EPS) * og_ref[...] * (1.0 - lam_init)
    o_ref[...] = o.astype(o_ref.dtype)


def _attn_scratch(n):
    out = []
    for _ in range(n):
        out += [pltpu.VMEM((ATT_TILE, 1), jnp.float32), pltpu.VMEM((ATT_TILE, 1), jnp.float32),
                pltpu.VMEM((ATT_TILE, HEAD_DIM), jnp.float32)]
    return out


def diff_attention(qk, vall, lam_params, out_gain, bsz, seq, lam_init):
    t = bsz * seq
    nq = seq // ATT_TILE
    row = pl.BlockSpec((1, A_QK), lambda b, h, i: (0, 0))
    return pl.pallas_call(
        functools.partial(_diff_attn_kernel, lam_init=lam_init),
        grid=(bsz, A_HEADS, nq),
        in_specs=[pl.BlockSpec((ATT_TILE, HEAD_DIM), lambda b, h, i: (b * nq + i, h)),
                  pl.BlockSpec((seq, HEAD_DIM), lambda b, h, i: (b, A_HEADS + h)),
                  pl.BlockSpec((seq, HEAD_DIM), lambda b, h, i: (b, h)),
                  row, row, row, row,
                  pl.BlockSpec((1, HEAD_DIM), lambda b, h, i: (0, 0))],
        out_specs=pl.BlockSpec((ATT_TILE, HEAD_DIM), lambda b, h, i: (b * nq + i, h)),
        out_shape=jax.ShapeDtypeStruct((t, A_HEADS * HEAD_DIM), jnp.bfloat16),
        scratch_shapes=_attn_scratch(2),
        compiler_params=_params("parallel", "parallel", "arbitrary"),
        name="diff_attention",
    )(qk, qk, vall, *[p.reshape(1, A_QK) for p in lam_params], out_gain.reshape(1, HEAD_DIM))


def _fox_attn_kernel(q_ref, k_ref, v_ref, cq_ref, ck_ref, o_ref, m_ref, l_ref, acc_ref):
    h = pl.program_id(1)
    qi = pl.program_id(2)
    scale = HEAD_DIM ** -0.5
    q = q_ref[...]
    cum = cq_ref[...]
    lane = lax.broadcasted_iota(jnp.int32, cum.shape, 1)
    cq = jnp.sum(jnp.where(lane == h, cum, 0.0), axis=-1, keepdims=True)
    _init_softmax(m_ref, l_ref, acc_ref)

    def step(j, masked):
        kb = _kv_block(k_ref, j)
        vb = _kv_block(v_ref, j)
        ck = ck_ref[0, :, pl.ds(pl.multiple_of(j * ATT_TILE, ATT_TILE), ATT_TILE)]
        s = lax.dot_general(q, kb, _NT, preferred_element_type=jnp.float32) * scale
        s = s + cq - ck
        if masked:
            s = _causal_mask(s)
        _softmax_step(s, vb, m_ref, l_ref, acc_ref)

    def body(j, c):
        step(j, False)
        return c

    lax.fori_loop(0, qi, body, 0)
    step(qi, True)
    o_ref[...] = (acc_ref[...] / l_ref[...]).astype(o_ref.dtype)


def fox_attention(qk, vall, cum, cumt, bsz, seq, v_block0):
    t = bsz * seq
    nq = seq // ATT_TILE
    return pl.pallas_call(
        _fox_attn_kernel,
        grid=(bsz, C_HEADS, nq),
        in_specs=[pl.BlockSpec((ATT_TILE, HEAD_DIM), lambda b, h, i: (b * nq + i, h)),
                  pl.BlockSpec((seq, HEAD_DIM), lambda b, h, i: (b, C_HEADS + h)),
                  pl.BlockSpec((seq, HEAD_DIM), lambda b, h, i: (b, v_block0 + h)),
                  pl.BlockSpec((ATT_TILE, LANES), lambda b, h, i: (b * nq + i, 0)),
                  pl.BlockSpec((1, 1, seq), lambda b, h, i: (h, 0, b))],
        out_specs=pl.BlockSpec((ATT_TILE, HEAD_DIM), lambda b, h, i: (b * nq + i, h)),
        out_shape=jax.ShapeDtypeStruct((t, C_HEADS * HEAD_DIM), jnp.bfloat16),
        scratch_shapes=_attn_scratch(1),
        compiler_params=_params("parallel", "parallel", "arbitrary"),
        name="fox_attention",
    )(qk, qk, vall, cum, cumt.reshape(8, 1, t))


def _moba_attn_kernel(q_ref, k_ref, v_ref, kmean_ref, o_ref, m_ref, l_ref, acc_ref, sel_ref):
    qi = pl.program_id(2)
    scale = HEAD_DIM ** -0.5
    q = q_ref[...]
    nb = kmean_ref.shape[0]
    gate = lax.dot_general(q, kmean_ref[...].astype(q.dtype), _NT,
                           preferred_element_type=jnp.float32)
    lane = lax.broadcasted_iota(jnp.int32, gate.shape, 1)
    g = jnp.where(lane < qi, gate, NEG_INF)
    sel = jnp.zeros(gate.shape, jnp.float32)
    for _ in range(MOBA_TOPK):
        mx = jnp.max(g, axis=-1, keepdims=True)
        idx = jnp.min(jnp.where(g == mx, lane, nb), axis=-1, keepdims=True)
        hit = lane == jnp.where(mx > NEG_INF, idx, nb)
        sel = jnp.where(hit, 1.0, sel)
        g = jnp.where(hit, NEG_INF, g)
    sel_ref[...] = sel
    _init_softmax(m_ref, l_ref, acc_ref)

    s = lax.dot_general(q, _kv_block(k_ref, qi), _NT, preferred_element_type=jnp.float32) * scale
    _softmax_step(_causal_mask(s), _kv_block(v_ref, qi), m_ref, l_ref, acc_ref)

    def body(j, c):
        selm = sel_ref[...]
        ln = lax.broadcasted_iota(jnp.int32, selm.shape, 1)
        chosen = jnp.sum(jnp.where(ln == j, selm, 0.0), axis=-1, keepdims=True)
        s = lax.dot_general(q, _kv_block(k_ref, j), _NT, preferred_element_type=jnp.float32) * scale
        s = jnp.where(chosen > 0.0, s, NEG_INF)
        _softmax_step(s, _kv_block(v_ref, j), m_ref, l_ref, acc_ref)
        return c

    lax.fori_loop(0, qi, body, 0)
    o_ref[...] = (acc_ref[...] / l_ref[...]).astype(o_ref.dtype)


def moba_attention(qk, vall, kmean, bsz, seq, v_block0):
    assert ATT_TILE == MOBA_BLOCK
    t = bsz * seq
    nq = seq // ATT_TILE
    nb = seq // MOBA_BLOCK
    return pl.pallas_call(
        _moba_attn_kernel,
        grid=(bsz, B_HEADS, nq),
        in_specs=[pl.BlockSpec((ATT_TILE, HEAD_DIM), lambda b, h, i: (b * nq + i, h)),
                  pl.BlockSpec((seq, HEAD_DIM), lambda b, h, i: (b, B_HEADS + h)),
                  pl.BlockSpec((seq, HEAD_DIM), lambda b, h, i: (b, v_block0 + h)),
                  pl.BlockSpec((nb, HEAD_DIM), lambda b, h, i: (b, B_HEADS + h))],
        out_specs=pl.BlockSpec((ATT_TILE, HEAD_DIM), lambda b, h, i: (b * nq + i, h)),
        out_shape=jax.ShapeDtypeStruct((t, B_HEADS * HEAD_DIM), jnp.bfloat16),
        scratch_shapes=_attn_scratch(1) + [pltpu.VMEM((ATT_TILE, nb), jnp.float32)],
        compiler_params=_params("parallel", "parallel", "arbitrary"),
        name="moba_attention",
    )(qk, qk, vall, kmean)


def _merge_kernel(ya_ref, yb_ref, yc_ref, g_ref, wa_ref, wb_ref, wc_ref, o_ref):
    d = o_ref.shape[1]
    pa = jnp.dot(ya_ref[...], wa_ref[...], preferred_element_type=jnp.float32)
    pb = jnp.dot(yb_ref[...], wb_ref[...], preferred_element_type=jnp.float32)
    pc = jnp.dot(yc_ref[...], wc_ref[...], preferred_element_type=jnp.float32)
    merged = (g_ref[:, 0:d].astype(jnp.float32) * pa
              + g_ref[:, d:2 * d].astype(jnp.float32) * pb
              + g_ref[:, 2 * d:3 * d].astype(jnp.float32) * pc)
    o_ref[...] = merged.astype(o_ref.dtype)


def gated_merge(ya, yb, yc, gates, wa, wb, wc):
    t = ya.shape[0]
    d = wa.shape[1]
    tm = min(512, t)

    def rows(n):
        return pl.BlockSpec((tm, n), lambda i: (i, 0))

    def whole(w):
        return pl.BlockSpec(w.shape, lambda i: (0, 0))

    return pl.pallas_call(
        _merge_kernel,
        grid=(t // tm,),
        in_specs=[rows(ya.shape[1]), rows(yb.shape[1]), rows(yc.shape[1]), rows(gates.shape[1]),
                  whole(wa), whole(wb), whole(wc)],
        out_specs=rows(d),
        out_shape=jax.ShapeDtypeStruct((t, d), jnp.bfloat16),
        compiler_params=_params("parallel"),
        name="gated_merge",
    )(ya, yb, yc, gates, wa, wb, wc)


def _peer_scores_kernel(h_ref, w_ref, keys_ref, s_ref):
    half = PEER_KEYS
    q = jnp.dot(h_ref[...], w_ref[...], preferred_element_type=jnp.float32).astype(jnp.bfloat16)
    for p in range(2):
        s_ref[0, p] = lax.dot_general(keys_ref[0, p], q[:, p * half:(p + 1) * half], _NT,
                                      preferred_element_type=jnp.float32)


def peer_scores(h, w_q, sub_keys):
    t, k = h.shape
    qd = 2 * PEER_KEYS
    tm = min(1024, t)
    return pl.pallas_call(
        _peer_scores_kernel,
        grid=(t // tm, PEER_HEADS),
        in_specs=[pl.BlockSpec((tm, k), lambda i, j: (i, 0)),
                  pl.BlockSpec((k, qd), lambda i, j: (0, j)),
                  pl.BlockSpec((1, 2, PEER_KEYS, PEER_KEYS), lambda i, j: (j, 0, 0, 0))],
        out_specs=pl.BlockSpec((1, 2, PEER_KEYS, tm), lambda i, j: (j, 0, 0, i)),
        out_shape=jax.ShapeDtypeStruct((PEER_HEADS, 2, PEER_KEYS, t), jnp.float32),
        compiler_params=_params("parallel", "parallel"),
        name="peer_scores",
    )(h, w_q, sub_keys)


def _top_values(s, k):
    rows = lax.broadcasted_iota(jnp.int32, (k, s.shape[1]), 0)
    vals = jnp.full((k, s.shape[1]), NEG_INF, jnp.float32)
    for r in range(k):
        mx = jnp.max(s, axis=0, keepdims=True)
        vals = jnp.where(rows == r, mx, vals)
        s = jnp.where(s == mx, NEG_INF, s)
    return vals


def _peer_select_kernel(s_ref, a1_ref, e2_ref, tau_ref):
    s1 = s_ref[0, 0]
    s2 = s_ref[0, 1]
    k = PEER_TOPK
    v1 = _top_values(s1, k)
    v2 = _top_values(s2, k)
    cand = jnp.concatenate(
        [v1[a:a + 1] + (v2 if a == 0 else v2[:k // 2]) for a in range(k)], axis=0)
    c = cand
    tau = None
    for _ in range(k):
        tau = jnp.max(c, axis=0, keepdims=True)
        c = jnp.where(c == tau, NEG_INF, c)
    top = v1[0:1] + v2[0:1]
    z = jnp.sum(jnp.where(cand >= tau, jnp.exp(cand - top), 0.0), axis=0, keepdims=True)
    a1_ref[0] = jnp.exp(s1 - v1[0:1]) / z
    e2_ref[0] = jnp.exp(s2 - v2[0:1])
    tau_ref[0] = tau


def peer_select(scores):
    nh, _, nk, t = scores.shape
    tt = min(256, t)
    fac = jax.ShapeDtypeStruct((nh, nk, t), jnp.float32)
    fac_spec = pl.BlockSpec((1, nk, tt), lambda h, i: (h, 0, i))
    return pl.pallas_call(
        _peer_select_kernel,
        grid=(nh, t // tt),
        in_specs=[pl.BlockSpec((1, 2, nk, tt), lambda h, i: (h, 0, 0, i))],
        out_specs=[fac_spec, fac_spec, pl.BlockSpec((1, 1, tt), lambda h, i: (h, 0, i))],
        out_shape=[fac, fac, jax.ShapeDtypeStruct((nh, 1, t), jnp.float32)],
        compiler_params=_params("parallel", "parallel"),
        name="peer_select",
    )(scores)


PEER_TOKEN_TILE = 512
PEER_EXPERT_TILE = 512
PEER_LANE_CHUNK = 256


def _peer_ffn_kernel(x_ref, u_ref, v_ref, s1_ref, s2_ref, a1_ref, e2_ref, tau_ref, r_ref, o_ref,
                     acc_ref, xu_ref, p_ref):
    e = pl.program_id(1)
    et, tt = xu_ref.shape
    ipt = et // PEER_KEYS
    lc = min(PEER_LANE_CHUNK, tt)

    @pl.when(e == 0)
    def _():
        acc_ref[...] = jnp.zeros_like(acc_ref)

    xu_ref[...] = lax.dot_general(u_ref[...], x_ref[...], _NT,
                                  preferred_element_type=jnp.float32)

    def body(il, c):
        i = e * ipt + il
        rows = pl.ds(pl.multiple_of(il * PEER_KEYS, PEER_KEYS), PEER_KEYS)
        for tb in range(tt // lc):
            lanes = slice(tb * lc, (tb + 1) * lc)
            w = jnp.zeros((PEER_KEYS, lc), jnp.float32)
            for h in range(PEER_HEADS):
                ssum = s2_ref[h, 0, :, lanes] + s1_ref[h, 0, pl.ds(i, 1), lanes]
                w = w + jnp.where(ssum >= tau_ref[h, :, lanes],
                                  e2_ref[h, :, lanes] * a1_ref[h, pl.ds(i, 1), lanes], 0.0)
            xu = xu_ref[rows, lanes]
            act = 0.5 * xu * (1.0 + lax.erf(xu * math.sqrt(0.5)))
            p_ref[rows, lanes] = (act * w).astype(p_ref.dtype)
        return c

    lax.fori_loop(0, ipt, body, 0)
    acc_ref[...] += lax.dot_general(p_ref[...], v_ref[...], _TN,
                                    preferred_element_type=jnp.float32)

    @pl.when(e == pl.num_programs(1) - 1)
    def _():
        o_ref[...] = r_ref[...] + acc_ref[...]


def peer_ffn(h, u_tab, v_tab, scores, a1, e2, tau, resid):
    t, d = h.shape
    ne = u_tab.shape[0]
    tt = min(PEER_TOKEN_TILE, t)
    et = PEER_EXPERT_TILE
    nh, nk = PEER_HEADS, PEER_KEYS
    tok = pl.BlockSpec((tt, d), lambda i, e: (i, 0))
    exp = pl.BlockSpec((et, d), lambda i, e: (e, 0))
    fac = pl.BlockSpec((nh, nk, tt), lambda i, e: (0, 0, i))
    return pl.pallas_call(
        _peer_ffn_kernel,
        grid=(t // tt, ne // et),
        in_specs=[tok, exp, exp,
                  pl.BlockSpec((nh, 1, nk, tt), lambda i, e: (0, 0, 0, i)),
                  pl.BlockSpec((nh, 1, nk, tt), lambda i, e: (0, 1, 0, i)),
                  fac, fac,
                  pl.BlockSpec((nh, 1, tt), lambda i, e: (0, 0, i)),
                  tok],
        out_specs=tok,
        out_shape=jax.ShapeDtypeStruct((t, d), jnp.float32),
        scratch_shapes=[pltpu.VMEM((tt, d), jnp.float32),
                        pltpu.VMEM((et, tt), jnp.float32),
                        pltpu.VMEM((et, tt), jnp.bfloat16)],
        compiler_params=_params("parallel", "arbitrary"),
        name="peer_ffn",
    )(h, u_tab, v_tab, scores, scores, a1, e2, tau, resid)


def _rope_tables(seq, group):
    rd = group // ROPE_FRACTION
    half = rd // 2
    inv = ROPE_THETA ** (-(jnp.arange(half, dtype=jnp.float32) * 2.0) / rd)
    ang = jnp.arange(seq).astype(jnp.float32)[:, None] * inv[None, :]
    cos, sin = jnp.cos(ang), jnp.sin(ang)
    ones = jnp.ones((seq, group - rd), jnp.float32)
    zeros = jnp.zeros((seq, group - rd), jnp.float32)
    zh = jnp.zeros((seq, half), jnp.float32)
    reps = LANES // group
    c = jnp.tile(jnp.concatenate([cos, cos, ones], axis=1), (1, reps))
    sp = jnp.tile(jnp.concatenate([-sin, zh, zeros], axis=1), (1, reps))
    sm = jnp.tile(jnp.concatenate([zh, sin, zeros], axis=1), (1, reps))
    return c, sp, sm


def _layer(x2d, bsz, seq, layer_idx, p, tabs_a, tabs_b):
    bf = jnp.bfloat16
    w_in = p["w_in"]
    qa_w = 2 * A_HEADS * 2 * A_QK
    v_a, w_b, w_c = A_HEADS * HEAD_DIM, B_HEADS * HEAD_DIM, C_HEADS * HEAD_DIM
    o_va = qa_w
    o_qb = o_va + v_a
    o_vb = o_qb + 2 * w_b
    o_qc = o_vb + w_b
    o_vc = o_qc + 2 * w_c
    o_f = o_vc + w_c
    o_g = o_f + C_HEADS

    h1 = rmsnorm_bf16(x2d, p["norm1_g"])

    gain_a = jnp.concatenate([jnp.tile(p["a_q_norm"], 2 * A_HEADS), jnp.tile(p["a_k_norm"], 2 * A_HEADS)])
    qk_a = mm_qk(h1, w_in[:, :qa_w].astype(bf), gain_a.reshape(1, -1), tabs_a, seq,
                 group=A_QK, rope=True, with_kmean=False, tn=512)
    gain_b = jnp.concatenate([jnp.tile(p["b_q_norm"], B_HEADS), jnp.tile(p["b_k_norm"], B_HEADS)])
    qk_b, kmean_b = mm_qk(h1, w_in[:, o_qb:o_vb].astype(bf), gain_b.reshape(1, -1), tabs_b, seq,
                          group=HEAD_DIM, rope=True, with_kmean=True, tn=640)
    gain_c = jnp.concatenate([jnp.tile(p["c_q_norm"], C_HEADS), jnp.tile(p["c_k_norm"], C_HEADS)])
    qk_c = mm_qk(h1, w_in[:, o_qc:o_vc].astype(bf), gain_c.reshape(1, -1), tabs_b, seq,
                 group=HEAD_DIM, rope=False, with_kmean=False, tn=640)
    w_v = jnp.concatenate([w_in[:, o_va:o_qb], w_in[:, o_vb:o_qc], w_in[:, o_vc:o_f]], axis=1)
    vall = mm_plain(h1, w_v.astype(bf))
    w_f = jnp.pad(w_in[:, o_f:o_g], ((0, 0), (0, LANES - C_HEADS))).astype(bf)
    b_f = jnp.pad(p["b_forget"], (0, LANES - C_HEADS)).reshape(1, LANES)
    cum, cumt = mm_forget(h1, w_f, b_f, seq)
    gates = mm_gate(h1, w_in[:, o_g:].astype(bf), p["b_gate"].reshape(-1))

    lam_init = 0.8 - 0.6 * math.exp(-0.3 * layer_idx)
    ya = diff_attention(qk_a, vall, (p["lambda_q1"], p["lambda_k1"], p["lambda_q2"], p["lambda_k2"]),
                        p["a_out_norm"], bsz, seq, lam_init)
    yb = moba_attention(qk_b, vall, kmean_b, bsz, seq, v_block0=A_HEADS)
    yc = fox_attention(qk_c, vall, cum, cumt, bsz, seq, v_block0=A_HEADS + B_HEADS)

    merged = gated_merge(ya, yb, yc, gates, p["w_branch_a"].astype(bf), p["w_branch_b"].astype(bf),
                         p["w_branch_c"].astype(bf))
    x2d = mm_resid(merged, p["w_out"].astype(bf), x2d)

    h2 = rmsnorm_bf16(x2d, p["norm2_g"])
    scores = peer_scores(h2, p["w_peer_q"].astype(bf), p["peer_sub_keys"].astype(bf))
    a1, e2, tau = peer_select(scores)
    return peer_ffn(h2, p["peer_u"].astype(bf), p["peer_v"].astype(bf), scores, a1, e2, tau, x2d)


def kernel(x, norm1_g, w_in, b_gate, b_forget, a_q_norm, a_k_norm, lambda_q1, lambda_k1, lambda_q2,
           lambda_k2, a_out_norm, b_q_norm, b_k_norm, c_q_norm, c_k_norm, w_branch_a, w_branch_b,
           w_branch_c, w_out, norm2_g, w_peer_q, peer_sub_keys, peer_u, peer_v):
    stacked = dict(norm1_g=norm1_g, w_in=w_in, b_gate=b_gate, b_forget=b_forget, a_q_norm=a_q_norm,
                   a_k_norm=a_k_norm, lambda_q1=lambda_q1, lambda_k1=lambda_k1, lambda_q2=lambda_q2,
                   lambda_k2=lambda_k2, a_out_norm=a_out_norm, b_q_norm=b_q_norm, b_k_norm=b_k_norm,
                   c_q_norm=c_q_norm, c_k_norm=c_k_norm, w_branch_a=w_branch_a, w_branch_b=w_branch_b,
                   w_branch_c=w_branch_c, w_out=w_out, norm2_g=norm2_g, w_peer_q=w_peer_q,
                   peer_sub_keys=peer_sub_keys, peer_u=peer_u, peer_v=peer_v)
    bsz, seq, d = x.shape
    assert d == D_MODEL and seq % MOBA_BLOCK == 0
    tabs_a = _rope_tables(seq, A_QK)
    tabs_b = _rope_tables(seq, HEAD_DIM)
    x2d = x.reshape(bsz * seq, d)
    for layer_idx in range(norm1_g.shape[0]):
        p = {name: val[layer_idx] for name, val in stacked.items()}
        x2d = _layer(x2d, bsz, seq, layer_idx, p, tabs_a, tabs_b)
    return x2d.reshape(bsz, seq, d)
```

```python
import functools
import math

import jax
import jax.numpy as jnp
from jax import lax
from jax.experimental import pallas as pl
from jax.experimental.pallas import tpu as pltpu

D_MODEL = 2048
HEAD_DIM = 128
NORM_EPS = 1e-6
ROPE_THETA = 500000.0
ROPE_FRACTION = 4
A_HEADS = 6
A_QK = 64
B_HEADS = 5
C_HEADS = 5
MOBA_BLOCK = 256
MOBA_TOPK = 3
N_BRANCH = 3
PEER_HEADS = 8
PEER_KEYS = 128
PEER_TOPK = 16

LANES = 128
VMEM_LIMIT = 56 * 1024 * 1024
ATT_TILE = 256
NEG_INF = float("-inf")
BF16 = jnp.bfloat16

BLK_QA, BLK_KA, BLK_VA = 0, A_HEADS, 2 * A_HEADS
BLK_QB = 3 * A_HEADS
BLK_KB, BLK_VB = BLK_QB + B_HEADS, BLK_QB + 2 * B_HEADS
BLK_QC = BLK_QB + 3 * B_HEADS
BLK_KC, BLK_VC = BLK_QC + C_HEADS, BLK_QC + 2 * C_HEADS
N_HEAD_BLOCKS = BLK_QC + 3 * C_HEADS
COL_F = N_HEAD_BLOCKS * LANES
COL_G = COL_F + C_HEADS

_NT = (((1,), (1,)), ((), ()))
_TN = (((0,), (0,)), ((), ()))


def _params(*sem):
    return pltpu.CompilerParams(dimension_semantics=sem, vmem_limit_bytes=VMEM_LIMIT)


def _rmsnorm_kernel(x_ref, g_ref, o_ref):
    x = x_ref[...]
    ms = jnp.mean(x * x, axis=-1, keepdims=True)
    o_ref[...] = (x * lax.rsqrt(ms + NORM_EPS) * g_ref[...]).astype(o_ref.dtype)


def rmsnorm_bf16(x2d, g):
    t, d = x2d.shape
    tm = min(512, t)
    return pl.pallas_call(
        _rmsnorm_kernel,
        grid=(t // tm,),
        in_specs=[pl.BlockSpec((tm, d), lambda i: (i, 0)),
                  pl.BlockSpec((1, d), lambda i: (0, 0))],
        out_specs=pl.BlockSpec((tm, d), lambda i: (i, 0)),
        out_shape=jax.ShapeDtypeStruct((t, d), BF16),
        compiler_params=_params("parallel"),
        name="rmsnorm",
    )(x2d, g.reshape(1, d))


def _cast_and_dot(h_ref, w_ref, wbf_ref):
    @pl.when(pl.program_id(1) == 0)
    def _():
        w = w_ref[0] if len(w_ref.shape) == 3 else w_ref[...]
        wbf_ref[...] = w.astype(wbf_ref.dtype)

    return jnp.dot(h_ref[...], wbf_ref[...], preferred_element_type=jnp.float32)


def _w_spec(w, layer, k, tn, col_block0=0):
    if w.ndim == 2:
        return pl.BlockSpec((k, tn), lambda j, i: (0, col_block0 + j))
    return pl.BlockSpec((1, k, tn), lambda j, i: (layer, 0, col_block0 + j))


def _mm_call(body, h, w, layer, n, tn, extra_in, extra_specs, out_shape, out_specs, name, tm=None):
    t, k = h.shape
    tm = min(1024, t) if tm is None else tm
    assert t % tm == 0 and n % tn == 0
    return pl.pallas_call(
        body,
        grid=(n // tn, t // tm),
        in_specs=[pl.BlockSpec((tm, k), lambda j, i: (i, 0)), _w_spec(w, layer, k, tn)] + extra_specs,
        out_specs=out_specs,
        out_shape=out_shape,
        scratch_shapes=[pltpu.VMEM((k, tn), BF16)],
        compiler_params=_params("arbitrary", "arbitrary"),
        name=name,
    )(h, w, *extra_in)


def _qk_epilogue(y, gain, group, rope, tabs):
    ss = y * y
    if group == LANES:
        ms = jnp.mean(ss, axis=-1, keepdims=True)
    else:
        lane = lax.broadcasted_iota(jnp.int32, y.shape, 1)
        lo = lane < group
        s_lo = jnp.sum(jnp.where(lo, ss, 0.0), axis=-1, keepdims=True)
        s_hi = jnp.sum(jnp.where(lo, 0.0, ss), axis=-1, keepdims=True)
        ms = jnp.where(lo, s_lo, s_hi) * (1.0 / group)
    yn = y * lax.rsqrt(ms + NORM_EPS) * gain
    if rope:
        cos_ref, sp_ref, sm_ref = tabs
        half = group // ROPE_FRACTION // 2
        yn = (yn * cos_ref[...] + pltpu.roll(yn, LANES - half, 1) * sp_ref[...]
              + pltpu.roll(yn, half, 1) * sm_ref[...])
    return yn


def _in_proj_kernel(h_ref, w_ref, g_ref, ca_ref, spa_ref, sma_ref, cb_ref, spb_ref, smb_ref,
                    o_ref, kmean_ref, wbf_ref):
    j = pl.program_id(0)
    acc = _cast_and_dot(h_ref, w_ref, wbf_ref)
    tm, tn = acc.shape
    nblk = tn // LANES
    tabs_a = (ca_ref, spa_ref, sma_ref)
    tabs_b = (cb_ref, spb_ref, smb_ref)
    for b in range(nblk):
        cols = slice(b * LANES, (b + 1) * LANES)
        bid = j * nblk + b
        y = acc[:, cols]
        is_a = bid < BLK_VA
        is_b = jnp.logical_and(bid >= BLK_QB, bid < BLK_VB)
        is_c = jnp.logical_and(bid >= BLK_QC, bid < BLK_VC)
        is_v = jnp.logical_not(jnp.logical_or(is_a, jnp.logical_or(is_b, is_c)))

        @pl.when(is_a)
        def _():
            o_ref[:, cols] = _qk_epilogue(y, g_ref[:, cols], A_QK, True, tabs_a).astype(o_ref.dtype)

        @pl.when(is_b)
        def _():
            yn = _qk_epilogue(y, g_ref[:, cols], HEAD_DIM, True, tabs_b)
            o_ref[:, cols] = yn.astype(o_ref.dtype)
            for r in range(tm // MOBA_BLOCK):
                kmean_ref[0, r:r + 1, cols] = jnp.mean(
                    yn[r * MOBA_BLOCK:(r + 1) * MOBA_BLOCK], axis=0, keepdims=True)

        @pl.when(is_c)
        def _():
            o_ref[:, cols] = _qk_epilogue(y, g_ref[:, cols], HEAD_DIM, False, None).astype(o_ref.dtype)

        @pl.when(is_v)
        def _():
            o_ref[:, cols] = y.astype(o_ref.dtype)

        @pl.when(jnp.logical_not(is_b))
        def _():
            kmean_ref[0, :, cols] = jnp.zeros((tm // MOBA_BLOCK, LANES), jnp.float32)


def in_proj(h, w_in, layer, gain_row, tabs_a, tabs_b, seq):
    t, k = h.shape
    n = N_HEAD_BLOCKS * LANES
    tn = 512
    tm = min(1024, seq)
    assert seq % tm == 0 and tm % MOBA_BLOCK == 0
    spt = seq // tm
    nb = tm // MOBA_BLOCK
    tab_spec = pl.BlockSpec((tm, LANES), lambda j, i: (i % spt, 0))
    out, kmean = _mm_call(
        _in_proj_kernel, h, w_in, layer, n, tn,
        [gain_row, *tabs_a, *tabs_b],
        [pl.BlockSpec((1, tn), lambda j, i: (0, j))] + [tab_spec] * 6,
        [jax.ShapeDtypeStruct((t, n), BF16), jax.ShapeDtypeStruct((t // tm, nb, n), jnp.float32)],
        [pl.BlockSpec((tm, tn), lambda j, i: (i, j)), pl.BlockSpec((1, nb, tn), lambda j, i: (i, 0, j))],
        "in_proj", tm=tm)
    return out, kmean.reshape(t // MOBA_BLOCK, n)


def _mm_gate_kernel(h_ref, w_ref, b_ref, o_ref, wbf_ref):
    acc = _cast_and_dot(h_ref, w_ref, wbf_ref)
    o_ref[...] = jax.nn.sigmoid(acc + b_ref[...]).astype(o_ref.dtype)


def mm_gate(h, w, b):
    t = h.shape[0]
    n = w.shape[-1]
    tn = 512
    tm = min(1024, t)
    return _mm_call(_mm_gate_kernel, h, w, 0, n, tn, [b.reshape(1, n)],
                    [pl.BlockSpec((1, tn), lambda j, i: (0, j))],
                    jax.ShapeDtypeStruct((t, n), BF16),
                    pl.BlockSpec((tm, tn), lambda j, i: (i, j)), "mm_gate")


def _mm_resid_kernel(h_ref, w_ref, r_ref, o_ref, wbf_ref):
    o_ref[...] = r_ref[...] + _cast_and_dot(h_ref, w_ref, wbf_ref)


def mm_resid(h, w, layer, resid):
    t = h.shape[0]
    n = w.shape[-1]
    tn = 512
    tm = min(1024, t)
    tile = pl.BlockSpec((tm, tn), lambda j, i: (i, j))
    return _mm_call(_mm_resid_kernel, h, w, layer, n, tn, [resid], [tile],
                    jax.ShapeDtypeStruct((t, n), jnp.float32), tile, "mm_resid")


def _mm_forget_kernel(h_ref, w_ref, b_ref, cum_ref, cumt_ref, carry_ref, *, spt):
    i = pl.program_id(0)

    @pl.when(i % spt == 0)
    def _():
        carry_ref[...] = jnp.zeros_like(carry_ref)

    acc = jnp.dot(h_ref[...], w_ref[...], preferred_element_type=jnp.float32)
    z = acc + b_ref[...]
    logf = jnp.minimum(z, 0.0) - jnp.log1p(jnp.exp(-jnp.abs(z)))
    tm = logf.shape[0]
    blk = MOBA_BLOCK
    row = lax.broadcasted_iota(jnp.int32, (blk, blk), 0)
    col = lax.broadcasted_iota(jnp.int32, (blk, blk), 1)
    tri = (col <= row).astype(jnp.float32)
    carry = carry_ref[...]
    for r in range(tm // blk):
        c = jnp.dot(tri, logf[r * blk:(r + 1) * blk], preferred_element_type=jnp.float32,
                    precision=lax.Precision.HIGHEST) + carry
        cum_ref[r * blk:(r + 1) * blk, :] = c
        cumt_ref[:, r * blk:(r + 1) * blk] = c.T[:8]
        carry = c[blk - 1:blk]
    carry_ref[...] = carry


def mm_forget(h, w_pad, b_pad, seq):
    t, k = h.shape
    tm = min(1024, seq)
    spt = seq // tm
    return pl.pallas_call(
        functools.partial(_mm_forget_kernel, spt=spt),
        grid=(t // tm,),
        in_specs=[pl.BlockSpec((tm, k), lambda i: (i, 0)),
                  pl.BlockSpec((k, LANES), lambda i: (0, 0)),
                  pl.BlockSpec((1, LANES), lambda i: (0, 0))],
        out_specs=[pl.BlockSpec((tm, LANES), lambda i: (i, 0)),
                   pl.BlockSpec((8, tm), lambda i: (0, i))],
        out_shape=[jax.ShapeDtypeStruct((t, LANES), jnp.float32),
                   jax.ShapeDtypeStruct((8, t), jnp.float32)],
        scratch_shapes=[pltpu.VMEM((1, LANES), jnp.float32)],
        compiler_params=_params("arbitrary"),
        name="mm_forget",
    )(h, w_pad, b_pad)


def _lane_fold(s, op):
    acc = s[:, :LANES]
    for c in range(1, s.shape[1] // LANES):
        acc = op(acc, s[:, c * LANES:(c + 1) * LANES])
    return acc


def _softmax_step(s, v, m_ref, l_ref, acc_ref):
    m_old = m_ref[...]
    m_new = jnp.maximum(m_old, jnp.max(_lane_fold(s, jnp.maximum), axis=-1, keepdims=True))
    alpha = jnp.exp(m_old - m_new)
    ps = [jnp.exp(s[:, c * LANES:(c + 1) * LANES] - m_new) for c in range(s.shape[1] // LANES)]
    psum = ps[0]
    for p in ps[1:]:
        psum = psum + p
    l_ref[...] = alpha * l_ref[...] + jnp.sum(psum, axis=-1, keepdims=True)
    p = jnp.concatenate(ps, axis=1).astype(v.dtype)
    acc_ref[...] = alpha * acc_ref[...] + jnp.dot(p, v, preferred_element_type=jnp.float32)
    m_ref[...] = m_new


def _init_softmax(m_ref, l_ref, acc_ref):
    m_ref[...] = jnp.full(m_ref.shape, NEG_INF, jnp.float32)
    l_ref[...] = jnp.zeros(l_ref.shape, jnp.float32)
    acc_ref[...] = jnp.zeros(acc_ref.shape, jnp.float32)


def _causal_mask(s):
    row = lax.broadcasted_iota(jnp.int32, s.shape, 0)
    col = lax.broadcasted_iota(jnp.int32, s.shape, 1)
    return jnp.where(col <= row, s, NEG_INF)


def _kv_block(ref, j):
    return ref[pl.ds(pl.multiple_of(j * ATT_TILE, ATT_TILE), ATT_TILE), :]


def _chains(scratch, n):
    return [tuple(scratch[3 * c:3 * c + 3]) for c in range(n)]


def _attn_scratch(n):
    out = []
    for _ in range(n):
        out += [pltpu.VMEM((ATT_TILE, LANES), jnp.float32), pltpu.VMEM((ATT_TILE, LANES), jnp.float32),
                pltpu.VMEM((ATT_TILE, HEAD_DIM), jnp.float32)]
    return out


def _head_specs(blk0, heads, seq, nq):
    q = [pl.BlockSpec((ATT_TILE, HEAD_DIM), functools.partial(
        lambda b, hg, i, g: (b * nq + i, blk0[0] + hg * heads + g), g=g)) for g in range(heads)]
    kv = [pl.BlockSpec((seq, HEAD_DIM), functools.partial(
        lambda b, hg, i, g, o: (b, o + hg * heads + g), g=g, o=o))
          for o in blk0[1:] for g in range(heads)]
    return q + kv


def _diff_attn_kernel(*refs, heads, lam_init):
    q_refs, k_refs, v_refs = refs[:heads], refs[heads:2 * heads], refs[2 * heads:3 * heads]
    lq1_ref, lk1_ref, lq2_ref, lk2_ref, og_ref, o_ref = refs[3 * heads:3 * heads + 6]
    chains = _chains(refs[3 * heads + 6:], 2 * heads)
    qi = pl.program_id(2)
    scale = A_QK ** -0.5
    qs = []
    for g in range(heads):
        q = q_refs[g][...]
        lane = lax.broadcasted_iota(jnp.int32, q.shape, 1)
        zero = jnp.zeros_like(q)
        qs += [jnp.where(lane < A_QK, q, zero), jnp.where(lane < A_QK, zero, q)]
    for c in chains:
        _init_softmax(*c)

    def step(j, masked):
        for g in range(heads):
            kb = _kv_block(k_refs[g], j)
            vb = _kv_block(v_refs[g], j)
            for half in range(2):
                s = lax.dot_general(qs[2 * g + half], kb, _NT,
                                    preferred_element_type=jnp.float32) * scale
                if masked:
                    s = _causal_mask(s)
                _softmax_step(s, vb, *chains[2 * g + half])

    def body(j, c):
        step(j, False)
        return c

    lax.fori_loop(0, qi, body, 0)
    step(qi, True)

    lam = (jnp.exp(jnp.sum(lq1_ref[...] * lk1_ref[...], axis=-1, keepdims=True))
           - jnp.exp(jnp.sum(lq2_ref[...] * lk2_ref[...], axis=-1, keepdims=True)) + lam_init)
    for g in range(heads):
        (_, l1, a1), (_, l2, a2) = chains[2 * g], chains[2 * g + 1]
        o = a1[...] / l1[...] - lam * (a2[...] / l2[...])
        ms = jnp.mean(o * o, axis=-1, keepdims=True)
        o = o * lax.rsqrt(ms + NORM_EPS) * og_ref[...] * (1.0 - lam_init)
        o_ref[:, g * HEAD_DIM:(g + 1) * HEAD_DIM] = o.astype(o_ref.dtype)


def diff_attention(proj, lam_params, out_gain, bsz, seq, lam_init, heads=3):
    t = bsz * seq
    nq = seq // ATT_TILE
    row = pl.BlockSpec((1, A_QK), lambda b, hg, i: (0, 0))
    return pl.pallas_call(
        functools.partial(_diff_attn_kernel, heads=heads, lam_init=lam_init),
        grid=(bsz, A_HEADS // heads, nq),
        in_specs=_head_specs((BLK_QA, BLK_KA, BLK_VA), heads, seq, nq)
        + [row, row, row, row, pl.BlockSpec((1, HEAD_DIM), lambda b, hg, i: (0, 0))],
        out_specs=pl.BlockSpec((ATT_TILE, heads * HEAD_DIM), lambda b, hg, i: (b * nq + i, hg)),
        out_shape=jax.ShapeDtypeStruct((t, A_HEADS * HEAD_DIM), BF16),
        scratch_shapes=_attn_scratch(2 * heads),
        compiler_params=_params("parallel", "parallel", "arbitrary"),
        name="diff_attention",
    )(*([proj] * (3 * heads)), *[p.reshape(1, A_QK) for p in lam_params],
      out_gain.reshape(1, HEAD_DIM))


def _fox_attn_kernel(*refs, heads):
    q_refs, k_refs, v_refs = refs[:heads], refs[heads:2 * heads], refs[2 * heads:3 * heads]
    cq_ref, ck_ref, o_ref = refs[3 * heads:3 * heads + 3]
    chains = _chains(refs[3 * heads + 3:], heads)
    cqb_refs = refs[3 * heads + 3 + 3 * heads:]
    qi = pl.program_id(2)
    scale = HEAD_DIM ** -0.5
    qs = [q_refs[g][...] for g in range(heads)]
    cum = cq_ref[...]
    for g in range(heads):
        cqb_refs[g][...] = jnp.broadcast_to(cum[:, g:g + 1], cum.shape)
    for c in chains:
        _init_softmax(*c)

    def step(j, masked):
        cols = pl.ds(pl.multiple_of(j * ATT_TILE, ATT_TILE), ATT_TILE)
        for g in range(heads):
            s = lax.dot_general(qs[g], _kv_block(k_refs[g], j), _NT,
                                preferred_element_type=jnp.float32) * scale
            bias = cqb_refs[g][...]
            s = s + jnp.concatenate([bias] * (ATT_TILE // LANES), axis=1) - ck_ref[g, :, cols]
            if masked:
                s = _causal_mask(s)
            _softmax_step(s, _kv_block(v_refs[g], j), *chains[g])

    def body(j, c):
        step(j, False)
        return c

    lax.fori_loop(0, qi, body, 0)
    step(qi, True)
    for g in range(heads):
        _, l, a = chains[g]
        o_ref[:, g * HEAD_DIM:(g + 1) * HEAD_DIM] = (a[...] / l[...]).astype(o_ref.dtype)


def fox_attention(proj, cum, cumt, bsz, seq, heads=C_HEADS):
    assert heads == C_HEADS
    t = bsz * seq
    nq = seq // ATT_TILE
    return pl.pallas_call(
        functools.partial(_fox_attn_kernel, heads=heads),
        grid=(bsz, 1, nq),
        in_specs=_head_specs((BLK_QC, BLK_KC, BLK_VC), heads, seq, nq)
        + [pl.BlockSpec((ATT_TILE, LANES), lambda b, hg, i: (b * nq + i, 0)),
           pl.BlockSpec((8, 1, seq), lambda b, hg, i: (0, 0, b))],
        out_specs=pl.BlockSpec((ATT_TILE, heads * HEAD_DIM), lambda b, hg, i: (b * nq + i, hg)),
        out_shape=jax.ShapeDtypeStruct((t, C_HEADS * HEAD_DIM), BF16),
        scratch_shapes=_attn_scratch(heads) + [pltpu.VMEM((ATT_TILE, LANES), jnp.float32)] * heads,
        compiler_params=_params("parallel", "parallel", "arbitrary"),
        name="fox_attention",
    )(*([proj] * (3 * heads)), cum, cumt.reshape(8, 1, t))


def _moba_attn_kernel(*refs, heads):
    q_refs, k_refs, v_refs = refs[:heads], refs[heads:2 * heads], refs[2 * heads:3 * heads]
    kmean_refs = refs[3 * heads:4 * heads]
    o_ref = refs[4 * heads]
    chains = _chains(refs[4 * heads + 1:], heads)
    sel_refs = refs[4 * heads + 1 + 3 * heads:]
    qi = pl.program_id(2)
    scale = HEAD_DIM ** -0.5
    qs = [q_refs[g][...] for g in range(heads)]
    for g in range(heads):
        nb = kmean_refs[g].shape[0]
        gate = lax.dot_general(qs[g], kmean_refs[g][...].astype(BF16), _NT,
                               preferred_element_type=jnp.float32)
        lane = lax.broadcasted_iota(jnp.int32, gate.shape, 1)
        gt = jnp.where(lane < qi, gate, NEG_INF)
        sel = jnp.zeros(gate.shape, jnp.float32)
        for _ in range(MOBA_TOPK):
            mx = jnp.max(gt, axis=-1, keepdims=True)
            idx = jnp.min(jnp.where(gt == mx, lane, nb), axis=-1, keepdims=True)
            hit = lane == jnp.where(mx > NEG_INF, idx, nb)
            sel = jnp.where(hit, 1.0, sel)
            gt = jnp.where(hit, NEG_INF, gt)
        sel_refs[g][...] = sel
        _init_softmax(*chains[g])
        s = lax.dot_general(qs[g], _kv_block(k_refs[g], qi), _NT,
                            preferred_element_type=jnp.float32) * scale
        _softmax_step(_causal_mask(s), _kv_block(v_refs[g], qi), *chains[g])

    def body(j, c):
        for g in range(heads):
            selm = sel_refs[g][...]
            ln = lax.broadcasted_iota(jnp.int32, selm.shape, 1)
            chosen = jnp.sum(jnp.where(ln == j, selm, 0.0), axis=-1, keepdims=True)
            s = lax.dot_general(qs[g], _kv_block(k_refs[g], j), _NT,
                                preferred_element_type=jnp.float32) * scale
            s = jnp.where(chosen > 0.0, s, NEG_INF)
            _softmax_step(s, _kv_block(v_refs[g], j), *chains[g])
        return c

    lax.fori_loop(0, qi, body, 0)
    for g in range(heads):
        _, l, a = chains[g]
        o_ref[:, g * HEAD_DIM:(g + 1) * HEAD_DIM] = (a[...] / l[...]).astype(o_ref.dtype)


def moba_attention(proj, kmean, bsz, seq, heads=B_HEADS):
    assert ATT_TILE == MOBA_BLOCK and heads == B_HEADS
    t = bsz * seq
    nq = seq // ATT_TILE
    nb = seq // MOBA_BLOCK
    kmean_specs = [pl.BlockSpec((nb, HEAD_DIM), functools.partial(
        lambda b, hg, i, g: (b, BLK_KB + g), g=g)) for g in range(heads)]
    return pl.pallas_call(
        functools.partial(_moba_attn_kernel, heads=heads),
        grid=(bsz, 1, nq),
        in_specs=_head_specs((BLK_QB, BLK_KB, BLK_VB), heads, seq, nq) + kmean_specs,
        out_specs=pl.BlockSpec((ATT_TILE, heads * HEAD_DIM), lambda b, hg, i: (b * nq + i, hg)),
        out_shape=jax.ShapeDtypeStruct((t, B_HEADS * HEAD_DIM), BF16),
        scratch_shapes=_attn_scratch(heads) + [pltpu.VMEM((ATT_TILE, nb), jnp.float32)] * heads,
        compiler_params=_params("parallel", "parallel", "arbitrary"),
        name="moba_attention",
    )(*([proj] * (3 * heads)), *([kmean] * heads))


def _merge_kernel(ya_ref, yb_ref, yc_ref, g_ref, wa_ref, wb_ref, wc_ref, o_ref, wa_bf, wb_bf, wc_bf):
    @pl.when(pl.program_id(0) == 0)
    def _():
        wa_bf[...] = wa_ref[0].astype(BF16)
        wb_bf[...] = wb_ref[0].astype(BF16)
        wc_bf[...] = wc_ref[0].astype(BF16)

    d = o_ref.shape[1]
    pa = jnp.dot(ya_ref[...], wa_bf[...], preferred_element_type=jnp.float32)
    pb = jnp.dot(yb_ref[...], wb_bf[...], preferred_element_type=jnp.float32)
    pc = jnp.dot(yc_ref[...], wc_bf[...], preferred_element_type=jnp.float32)
    merged = (g_ref[:, 0:d].astype(jnp.float32) * pa
              + g_ref[:, d:2 * d].astype(jnp.float32) * pb
              + g_ref[:, 2 * d:3 * d].astype(jnp.float32) * pc)
    o_ref[...] = merged.astype(o_ref.dtype)


def gated_merge(ya, yb, yc, gates, wa, wb, wc, layer):
    t = ya.shape[0]
    d = wa.shape[-1]
    tm = min(256, t)

    def rows(n):
        return pl.BlockSpec((tm, n), lambda i: (i, 0))

    def whole(w):
        return pl.BlockSpec((1,) + w.shape[1:], lambda i: (layer, 0, 0))

    return pl.pallas_call(
        _merge_kernel,
        grid=(t // tm,),
        in_specs=[rows(ya.shape[1]), rows(yb.shape[1]), rows(yc.shape[1]), rows(gates.shape[1]),
                  whole(wa), whole(wb), whole(wc)],
        out_specs=rows(d),
        out_shape=jax.ShapeDtypeStruct((t, d), BF16),
        scratch_shapes=[pltpu.VMEM(w.shape[1:], BF16) for w in (wa, wb, wc)],
        compiler_params=_params("arbitrary"),
        name="gated_merge",
    )(ya, yb, yc, gates, wa, wb, wc)


def _peer_scores_kernel(h_ref, w_ref, keys_ref, s_ref, wbf_ref):
    half = PEER_KEYS
    q = _cast_and_dot(h_ref, w_ref, wbf_ref).astype(BF16)
    for p in range(2):
        s_ref[0, p] = lax.dot_general(keys_ref[0, 0, p].astype(BF16), q[:, p * half:(p + 1) * half],
                                      _NT, preferred_element_type=jnp.float32)


def peer_scores(h, w_q, sub_keys, layer):
    t = h.shape[0]
    qd = 2 * PEER_KEYS
    tm = min(1024, t)
    return _mm_call(
        _peer_scores_kernel, h, w_q, layer, PEER_HEADS * qd, qd, [sub_keys],
        [pl.BlockSpec((1, 1, 2, PEER_KEYS, PEER_KEYS), lambda j, i: (layer, j, 0, 0, 0))],
        jax.ShapeDtypeStruct((PEER_HEADS, 2, PEER_KEYS, t), jnp.float32),
        pl.BlockSpec((1, 2, PEER_KEYS, tm), lambda j, i: (j, 0, 0, i)), "peer_scores")


def _top_values(s, k):
    rows = lax.broadcasted_iota(jnp.int32, (k, s.shape[1]), 0)
    vals = jnp.full((k, s.shape[1]), NEG_INF, jnp.float32)
    for r in range(k):
        mx = jnp.max(s, axis=0, keepdims=True)
        vals = jnp.where(rows == r, mx, vals)
        s = jnp.where(s == mx, NEG_INF, s)
    return vals


def _peer_select_kernel(s_ref, a1_ref, e2_ref, tau_ref):
    s1 = s_ref[0, 0]
    s2 = s_ref[0, 1]
    k = PEER_TOPK
    v1 = _top_values(s1, k)
    v2 = _top_values(s2, k)
    cand = jnp.concatenate(
        [v1[a:a + 1] + (v2 if a == 0 else v2[:k // 2]) for a in range(k)], axis=0)
    c = cand
    tau = None
    for _ in range(k):
        tau = jnp.max(c, axis=0, keepdims=True)
        c = jnp.where(c == tau, NEG_INF, c)
    top = v1[0:1] + v2[0:1]
    z = jnp.sum(jnp.where(cand >= tau, jnp.exp(cand - top), 0.0), axis=0, keepdims=True)
    a1_ref[0] = jnp.exp(s1 - v1[0:1]) / z
    e2_ref[0] = jnp.exp(s2 - v2[0:1])
    tau_ref[0] = tau


def peer_select(scores):
    nh, _, nk, t = scores.shape
    tt = min(256, t)
    fac = jax.ShapeDtypeStruct((nh, nk, t), jnp.float32)
    fac_spec = pl.BlockSpec((1, nk, tt), lambda h, i: (h, 0, i))
    return pl.pallas_call(
        _peer_select_kernel,
        grid=(nh, t // tt),
        in_specs=[pl.BlockSpec((1, 2, nk, tt), lambda h, i: (h, 0, 0, i))],
        out_specs=[fac_spec, fac_spec, pl.BlockSpec((1, 1, tt), lambda h, i: (h, 0, i))],
        out_shape=[fac, fac, jax.ShapeDtypeStruct((nh, 1, t), jnp.float32)],
        compiler_params=_params("parallel", "parallel"),
        name="peer_select",
    )(scores)


PEER_TOKEN_TILE = 512
PEER_EXPERT_TILE = 512
PEER_LANE_CHUNK = 256
PEER_ROW_CHUNK = 128


def _peer_ffn_kernel(x_ref, u_ref, v_ref, s1_ref, s2_ref, a1_ref, e2_ref, tau_ref, r_ref, o_ref,
                     acc_ref, xu_ref, p_ref):
    e = pl.program_id(1)
    et, tt = xu_ref.shape
    ipt = et // PEER_KEYS
    lc = min(PEER_LANE_CHUNK, tt)

    @pl.when(e == 0)
    def _():
        acc_ref[...] = jnp.zeros_like(acc_ref)

    xu_ref[...] = lax.dot_general(u_ref[...], x_ref[...], _NT,
                                  preferred_element_type=jnp.float32)
    rb = PEER_ROW_CHUNK
    for il in range(ipt):
        i = e * ipt + il
        for tb in range(tt // lc):
            lanes = slice(tb * lc, (tb + 1) * lc)
            for jb in range(PEER_KEYS // rb):
                keys = slice(jb * rb, (jb + 1) * rb)
                rows = slice(il * PEER_KEYS + jb * rb, il * PEER_KEYS + (jb + 1) * rb)
                w = jnp.zeros((rb, lc), jnp.float32)
                for h in range(PEER_HEADS):
                    ssum = s2_ref[h, 0, keys, lanes] + s1_ref[h, 0, pl.ds(i, 1), lanes]
                    w = w + jnp.where(ssum >= tau_ref[h, :, lanes],
                                      e2_ref[h, keys, lanes] * a1_ref[h, pl.ds(i, 1), lanes], 0.0)
                xu = xu_ref[rows, lanes]
                act = 0.5 * xu * (1.0 + lax.erf(xu * math.sqrt(0.5)))
                p_ref[rows, lanes] = (act * w).astype(p_ref.dtype)
    acc_ref[...] += lax.dot_general(p_ref[...], v_ref[...], _TN,
                                    preferred_element_type=jnp.float32)

    @pl.when(e == pl.num_programs(1) - 1)
    def _():
        o_ref[...] = r_ref[...] + acc_ref[...]


def peer_ffn(h, u_tab, v_tab, scores, a1, e2, tau, resid):
    t, d = h.shape
    ne = u_tab.shape[0]
    tt = min(PEER_TOKEN_TILE, t)
    et = PEER_EXPERT_TILE
    nh, nk = PEER_HEADS, PEER_KEYS
    tok = pl.BlockSpec((tt, d), lambda i, e: (i, 0))
    exp = pl.BlockSpec((et, d), lambda i, e: (e, 0))
    fac = pl.BlockSpec((nh, nk, tt), lambda i, e: (0, 0, i))
    return pl.pallas_call(
        _peer_ffn_kernel,
        grid=(t // tt, ne // et),
        in_specs=[tok, exp, exp,
                  pl.BlockSpec((nh, 1, nk, tt), lambda i, e: (0, 0, 0, i)),
                  pl.BlockSpec((nh, 1, nk, tt), lambda i, e: (0, 1, 0, i)),
                  fac, fac,
                  pl.BlockSpec((nh, 1, tt), lambda i, e: (0, 0, i)),
                  tok],
        out_specs=tok,
        out_shape=jax.ShapeDtypeStruct((t, d), jnp.float32),
        scratch_shapes=[pltpu.VMEM((tt, d), jnp.float32),
                        pltpu.VMEM((et, tt), jnp.float32),
                        pltpu.VMEM((et, tt), BF16)],
        compiler_params=_params("parallel", "arbitrary"),
        name="peer_ffn",
    )(h, u_tab, v_tab, scores, scores, a1, e2, tau, resid)


def _rope_tables(seq, group):
    rd = group // ROPE_FRACTION
    half = rd // 2
    inv = ROPE_THETA ** (-(jnp.arange(half, dtype=jnp.float32) * 2.0) / rd)
    ang = jnp.arange(seq).astype(jnp.float32)[:, None] * inv[None, :]
    cos, sin = jnp.cos(ang), jnp.sin(ang)
    ones = jnp.ones((seq, group - rd), jnp.float32)
    zeros = jnp.zeros((seq, group - rd), jnp.float32)
    zh = jnp.zeros((seq, half), jnp.float32)
    reps = LANES // group
    c = jnp.tile(jnp.concatenate([cos, cos, ones], axis=1), (1, reps))
    sp = jnp.tile(jnp.concatenate([-sin, zh, zeros], axis=1), (1, reps))
    sm = jnp.tile(jnp.concatenate([zh, sin, zeros], axis=1), (1, reps))
    return c, sp, sm


def _layer(x2d, bsz, seq, l, p, tabs_a, tabs_b):
    h1 = rmsnorm_bf16(x2d, p["norm1_g"][l])

    ones = jnp.ones((HEAD_DIM,), jnp.float32)
    gain_row = jnp.concatenate(
        [jnp.tile(p["a_q_norm"][l], 2 * A_HEADS), jnp.tile(p["a_k_norm"][l], 2 * A_HEADS),
         jnp.tile(ones, A_HEADS),
         jnp.tile(p["b_q_norm"][l], B_HEADS), jnp.tile(p["b_k_norm"][l], B_HEADS), jnp.tile(ones, B_HEADS),
         jnp.tile(p["c_q_norm"][l], C_HEADS), jnp.tile(p["c_k_norm"][l], C_HEADS), jnp.tile(ones, C_HEADS)])
    proj, kmean = in_proj(h1, p["w_in"], l, gain_row.reshape(1, -1), tabs_a, tabs_b, seq)

    w_in = p["w_in"][l]
    w_f = jnp.pad(w_in[:, COL_F:COL_G], ((0, 0), (0, LANES - C_HEADS))).astype(BF16)
    b_f = jnp.pad(p["b_forget"][l], (0, LANES - C_HEADS)).reshape(1, LANES)
    cum, cumt = mm_forget(h1, w_f, b_f, seq)
    gates = mm_gate(h1, w_in[:, COL_G:], p["b_gate"][l].reshape(-1))

    lam_init = 0.8 - 0.6 * math.exp(-0.3 * l)
    lam_params = tuple(p[n][l] for n in ("lambda_q1", "lambda_k1", "lambda_q2", "lambda_k2"))
    ya = diff_attention(proj, lam_params, p["a_out_norm"][l], bsz, seq, lam_init)
    yb = moba_attention(proj, kmean, bsz, seq)
    yc = fox_attention(proj, cum, cumt, bsz, seq)

    merged = gated_merge(ya, yb, yc, gates, p["w_branch_a"], p["w_branch_b"], p["w_branch_c"], l)
    x2d = mm_resid(merged, p["w_out"], l, x2d)

    h2 = rmsnorm_bf16(x2d, p["norm2_g"][l])
    scores = peer_scores(h2, p["w_peer_q"], p["peer_sub_keys"], l)
    a1, e2, tau = peer_select(scores)
    return peer_ffn(h2, p["peer_u"][l].astype(BF16), p["peer_v"][l].astype(BF16), scores, a1, e2, tau, x2d)


def kernel(x, norm1_g, w_in, b_gate, b_forget, a_q_norm, a_k_norm, lambda_q1, lambda_k1, lambda_q2,
           lambda_k2, a_out_norm, b_q_norm, b_k_norm, c_q_norm, c_k_norm, w_branch_a, w_branch_b,
           w_branch_c, w_out, norm2_g, w_peer_q, peer_sub_keys, peer_u, peer_v):
    p = dict(norm1_g=norm1_g, w_in=w_in, b_gate=b_gate, b_forget=b_forget, a_q_norm=a_q_norm,
             a_k_norm=a_k_norm, lambda_q1=lambda_q1, lambda_k1=lambda_k1, lambda_q2=lambda_q2,
             lambda_k2=lambda_k2, a_out_norm=a_out_norm, b_q_norm=b_q_norm, b_k_norm=b_k_norm,
             c_q_norm=c_q_norm, c_k_norm=c_k_norm, w_branch_a=w_branch_a, w_branch_b=w_branch_b,
             w_branch_c=w_branch_c, w_out=w_out, norm2_g=norm2_g, w_peer_q=w_peer_q,
             peer_sub_keys=peer_sub_keys, peer_u=peer_u, peer_v=peer_v)
    bsz, seq, d = x.shape
    assert d == D_MODEL and seq % MOBA_BLOCK == 0
    tabs_a = _rope_tables(seq, A_QK)
    tabs_b = _rope_tables(seq, HEAD_DIM)
    x2d = x.reshape(bsz * seq, d)
    for l in range(norm1_g.shape[0]):
        x2d = _layer(x2d, bsz, seq, l, p, tabs_a, tabs_b)
    return x2d.reshape(bsz, seq, d)
```
